```python
import math
import jax, jax.numpy as jnp
from jax import lax
import numpy as np

D_MODEL = 1024
BATCH = 2
SEQ = 8192
DEPTH = 4

SSD_EXPAND = 2
SSD_INNER = SSD_EXPAND * D_MODEL
SSD_HEADDIM = 64
SSD_HEADS = SSD_INNER // SSD_HEADDIM
SSD_GROUPS = 4
SSD_HEADS_PER_GROUP = SSD_HEADS // SSD_GROUPS
SSD_STATE = 128
SSD_CONV = 4
SSD_CHUNK = 256
SSD_CONV_DIM = SSD_INNER + 2 * SSD_GROUPS * SSD_STATE
DT_MIN = 1e-3
DT_MAX = 1e-1

ATTN_HEADS = 16
ATTN_HEAD_DIM = 64
ATTN_WIDTH = ATTN_HEADS * ATTN_HEAD_DIM
MOBA_BLOCK = 256
MOBA_TOPK = 3
Q_BLOCK = 128
ROPE_THETA = 10000.0

N_BRANCHES = 2
IN_SIZES = (SSD_INNER, SSD_CONV_DIM, SSD_HEADS, 3 * ATTN_WIDTH, N_BRANCHES * D_MODEL)
IN_COLS = sum(IN_SIZES)
IN_SPLITS = tuple(int(s) for s in np.cumsum(IN_SIZES)[:-1])

MOE_GROUPS = 4
MOE_EXPERTS_PER_GROUP = 8
N_EXPERTS = MOE_GROUPS * MOE_EXPERTS_PER_GROUP
MOE_TOPK = 2
EXPERT_FF = 512
MOE_BLOCK = 128

DEEPNORM_ALPHA = (2 * DEPTH) ** 0.25
DEEPNORM_BETA = (8 * DEPTH) ** -0.25
LN_EPS = 1e-5
NEG_INF = -1e30

kernel_name = "hybrid_ssd_moba_hmoe_deepnorm"


def layer_norm(x, g, b):
    xf = x.astype(jnp.float32)
    mu = jnp.mean(xf, axis=-1, keepdims=True)
    var = jnp.mean(jnp.square(xf - mu), axis=-1, keepdims=True)
    return ((xf - mu) * lax.rsqrt(var + LN_EPS) * g + b).astype(x.dtype)


def adaln(c, w, b):
    mod = jax.nn.silu(c) @ w + b
    shift, scale, gate = jnp.split(mod, 3, axis=-1)
    return shift[:, None, :], scale[:, None, :], gate[:, None, :]


def rope_tables(positions):
    inv_freq = ROPE_THETA ** (-jnp.arange(0, ATTN_HEAD_DIM, 2, dtype=jnp.float32) / ATTN_HEAD_DIM)
    ang = positions.astype(jnp.float32)[..., None] * inv_freq
    return jnp.cos(ang)[:, :, None, :], jnp.sin(ang)[:, :, None, :]


def apply_rope(x, cos, sin):
    xf = x.astype(jnp.float32)
    x1, x2 = jnp.split(xf, 2, axis=-1)
    return jnp.concatenate([x1 * cos - x2 * sin, x2 * cos + x1 * sin], axis=-1).astype(x.dtype)


def causal_depthwise_conv(x, w, b):
    out = lax.conv_general_dilated(
        x, w[:, None, :], window_strides=(1,), padding=[(SSD_CONV - 1, 0)],
        dimension_numbers=('NWC', 'WIO', 'NWC'), feature_group_count=x.shape[-1])
    return out + b


def ssd_chunked_scan(x, dt, a, b, c):
    bsz, seq, g, r, p = x.shape
    chunk = math.gcd(seq, SSD_CHUNK)
    nc = seq // chunk

    def to_chunks(t):
        return t.reshape((bsz, nc, chunk) + t.shape[2:]).swapaxes(0, 1)

    causal = jnp.tril(jnp.ones((chunk, chunk), dtype=bool))

    def step(state, inp):
        xc, dtc, bc, cc = inp
        xc = xc.astype(jnp.float32)
        bc = bc.astype(jnp.float32)
        cc = cc.astype(jnp.float32)
        cs = jnp.cumsum(dtc * a, axis=1)
        cs_t = jnp.moveaxis(cs, 1, -1)
        decay = jnp.exp(jnp.where(causal, cs_t[..., :, None] - cs_t[..., None, :], -jnp.inf))
        scores = jnp.einsum('blgn,bsgn->bgls', cc, bc)[:, :, None] * decay
        xdt = xc * dtc[..., None]
        y = jnp.einsum('bgrls,bsgrp->blgrp', scores, xdt)
        y = y + jnp.einsum('blgn,bgrpn->blgrp', cc, state) * jnp.exp(cs)[..., None]
        to_end = jnp.exp(cs[:, -1:] - cs)
        state = (state * jnp.exp(cs[:, -1])[..., None, None]
                 + jnp.einsum('bsgn,bsgrp->bgrpn', bc, xdt * to_end[..., None]))
        return state, y

    state0 = jnp.zeros((bsz, g, r, p, SSD_STATE), jnp.float32)
    _, y = lax.scan(step, state0, (to_chunks(x), to_chunks(dt), to_chunks(b), to_chunks(c)))
    return y.swapaxes(0, 1).reshape(bsz, seq, g, r, p)


def ssd_mixer(z, xbc, dt_raw, conv_w, conv_b, dt_bias, a_log, d_skip, norm_w):
    bsz, seq = z.shape[:2]
    xbc = jax.nn.silu(causal_depthwise_conv(xbc, conv_w, conv_b))
    xs, bs, cs = jnp.split(xbc, [SSD_INNER, SSD_INNER + SSD_GROUPS * SSD_STATE], axis=-1)
    xs = xs.reshape(bsz, seq, SSD_GROUPS, SSD_HEADS_PER_GROUP, SSD_HEADDIM)
    bs = bs.reshape(bsz, seq, SSD_GROUPS, SSD_STATE)
    cs = cs.reshape(bsz, seq, SSD_GROUPS, SSD_STATE)
    dt = jax.nn.softplus(dt_raw.astype(jnp.float32) + dt_bias.astype(jnp.float32))
    dt = dt.reshape(bsz, seq, SSD_GROUPS, SSD_HEADS_PER_GROUP)
    a = -jnp.exp(a_log.astype(jnp.float32)).reshape(SSD_GROUPS, SSD_HEADS_PER_GROUP)
    y = ssd_chunked_scan(xs, dt, a, bs, cs)
    y = y + d_skip.astype(jnp.float32).reshape(SSD_GROUPS, SSD_HEADS_PER_GROUP)[:, :, None] * xs.astype(jnp.float32)
    gy = y.reshape(bsz, seq, SSD_INNER) * jax.nn.silu(z.astype(jnp.float32))
    gy = gy.reshape(bsz, seq, SSD_GROUPS, SSD_INNER // SSD_GROUPS)
    gy = gy * lax.rsqrt(jnp.mean(jnp.square(gy), axis=-1, keepdims=True) + LN_EPS)
    return (gy.reshape(bsz, seq, SSD_INNER) * norm_w).astype(z.dtype)


def moba_attention(q, k, v):
    bsz, heads, seq, hd = q.shape
    nb = -(-seq // MOBA_BLOCK)
    pad = nb * MOBA_BLOCK - seq
    k_blocks = jnp.pad(k, ((0, 0), (0, 0), (0, pad), (0, 0))).reshape(bsz, heads, nb, MOBA_BLOCK, hd)
    v_blocks = jnp.pad(v, ((0, 0), (0, 0), (0, pad), (0, 0))).reshape(bsz, heads, nb, MOBA_BLOCK, hd)
    k_mean = jnp.mean(k_blocks.astype(jnp.float32), axis=3).astype(k.dtype)
    n_sel = min(MOBA_TOPK, nb)
    nq = seq // Q_BLOCK
    q_blocks = q.reshape(bsz, heads, nq, Q_BLOCK, hd).transpose(2, 0, 1, 3, 4)
    scale = hd ** -0.5
    b_ix = jnp.arange(bsz)[:, None, None, None]
    h_ix = jnp.arange(heads)[None, :, None, None]
    blk_ids = jnp.arange(nb)

    def attend(args):
        qi, qb = args
        own = (qi * Q_BLOCK) // MOBA_BLOCK
        gate = jnp.einsum('bhqd,bhnd->bhqn', qb, k_mean).astype(jnp.float32)
        gate = jnp.where(blk_ids < own, gate, -jnp.inf)
        _, sel = lax.top_k(gate, n_sel)
        valid = sel < own
        k_sel = k_blocks[b_ix, h_ix, sel]
        v_sel = v_blocks[b_ix, h_ix, sel]
        s_sel = jnp.einsum('bhqd,bhqnjd->bhqnj', qb, k_sel).astype(jnp.float32) * scale
        s_sel = jnp.where(valid[..., None], s_sel, NEG_INF)
        k_own = lax.dynamic_index_in_dim(k_blocks, own, axis=2, keepdims=False)
        v_own = lax.dynamic_index_in_dim(v_blocks, own, axis=2, keepdims=False)
        s_own = jnp.einsum('bhqd,bhjd->bhqj', qb, k_own).astype(jnp.float32) * scale
        q_pos = qi * Q_BLOCK + jnp.arange(Q_BLOCK)
        k_pos = own * MOBA_BLOCK + jnp.arange(MOBA_BLOCK)
        s_own = jnp.where(k_pos[None, :] <= q_pos[:, None], s_own, NEG_INF)
        s = jnp.concatenate([s_sel.reshape(bsz, heads, Q_BLOCK, n_sel * MOBA_BLOCK), s_own], axis=-1)
        p = jax.nn.softmax(s, axis=-1).astype(v.dtype)
        p_sel = p[..., :n_sel * MOBA_BLOCK].reshape(bsz, heads, Q_BLOCK, n_sel, MOBA_BLOCK)
        p_own = p[..., n_sel * MOBA_BLOCK:]
        return (jnp.einsum('bhqnj,bhqnjd->bhqd', p_sel, v_sel)
                + jnp.einsum('bhqj,bhjd->bhqd', p_own, v_own))

    out = lax.map(attend, (jnp.arange(nq), q_blocks))
    return out.transpose(1, 0, 3, 2, 4).reshape(bsz, seq, heads * hd)


def hybrid_mixer(h, cos, sin, w_in, conv_w, conv_b, dt_bias, a_log, d_skip, ssd_norm_w,
                 w_branch_ssd, w_branch_attn, w_out):
    bsz, seq, _ = h.shape
    proj = h @ w_in
    z, xbc, dt_raw, qkv, gates = jnp.split(proj, IN_SPLITS, axis=-1)
    y_ssd = ssd_mixer(z, xbc, dt_raw, conv_w, conv_b, dt_bias, a_log, d_skip, ssd_norm_w)
    q, k, v = jnp.split(qkv.reshape(bsz, seq, 3 * ATTN_HEADS, ATTN_HEAD_DIM), 3, axis=2)
    q = apply_rope(q, cos, sin).transpose(0, 2, 1, 3)
    k = apply_rope(k, cos, sin).transpose(0, 2, 1, 3)
    v = v.transpose(0, 2, 1, 3)
    y_attn = moba_attention(q, k, v)
    g_ssd, g_attn = jnp.split(gates, N_BRANCHES, axis=-1)
    merged = (jax.nn.sigmoid(g_ssd) * (y_ssd @ w_branch_ssd)
              + jax.nn.sigmoid(g_attn) * (y_attn @ w_branch_attn))
    return merged @ w_out


def hierarchical_moe(h, w_rg, b_rg, w_re, b_re, w_gate, w_up, w_down):
    bsz, seq, d = h.shape
    t = bsz * seq
    xt = h.reshape(t, d)
    g_prob = jax.nn.softmax((xt @ w_rg + b_rg).astype(jnp.float32), axis=-1)
    g_val, g_idx = lax.top_k(g_prob, 1)
    e_logits = (xt @ w_re + b_re).astype(jnp.float32).reshape(t, MOE_GROUPS, MOE_EXPERTS_PER_GROUP)
    e_in_group = jnp.take_along_axis(e_logits, g_idx[:, :, None], axis=1)[:, 0]
    e_val, e_idx = lax.top_k(e_in_group, MOE_TOPK)
    e_w = jax.nn.softmax(e_val, axis=-1) * g_val
    expert_id = g_idx * MOE_EXPERTS_PER_GROUP + e_idx

    n_assign = t * MOE_TOPK
    flat_e = expert_id.reshape(n_assign)
    flat_tok = jnp.repeat(jnp.arange(t, dtype=jnp.int32), MOE_TOPK)
    flat_w = e_w.reshape(n_assign)
    order = jnp.argsort(flat_e)
    se, stok, sw = flat_e[order], flat_tok[order], flat_w[order]
    counts = jnp.zeros((N_EXPERTS,), jnp.int32).at[flat_e].add(1)
    starts = jnp.cumsum(counts) - counts
    padded = (counts + MOE_BLOCK - 1) // MOE_BLOCK * MOE_BLOCK
    ends = jnp.cumsum(padded)
    pstarts = ends - padded
    dest = pstarts[se] + (jnp.arange(n_assign, dtype=jnp.int32) - starts[se])
    n_blocks = -(-n_assign // MOE_BLOCK) + N_EXPERTS
    n_rows = n_blocks * MOE_BLOCK
    row_tok = jnp.full((n_rows,), t, jnp.int32).at[dest].set(stok)
    row_w = jnp.zeros((n_rows,), jnp.float32).at[dest].set(sw)
    block_e = jnp.minimum(jnp.searchsorted(ends, jnp.arange(n_blocks) * MOE_BLOCK, side='right'),
                          N_EXPERTS - 1)
    x_pad = jnp.concatenate([xt, jnp.zeros((1, d), xt.dtype)], axis=0)
    x_rows = x_pad[row_tok].reshape(n_blocks, MOE_BLOCK, d)

    def run_expert(args):
        xb, e = args
        return (jax.nn.silu(xb @ w_gate[e]) * (xb @ w_up[e])) @ w_down[e]

    y_rows = lax.map(run_expert, (x_rows, block_e)).reshape(n_rows, d)
    y = jnp.zeros((t + 1, d), y_rows.dtype).at[row_tok].add(y_rows * row_w[:, None].astype(y_rows.dtype))
    return y[:t].reshape(bsz, seq, d)


def setup_inputs(seed: int = 0) -> dict:
    key = jax.random.key(seed)
    ks = jax.random.split(key, 27)

    def nrm(k, shape, std):
        return std * jax.random.normal(k, shape, jnp.float32)

    L, D = DEPTH, D_MODEL
    x = nrm(ks[0], (BATCH, SEQ, D), 1.0)
    c = nrm(ks[1], (BATCH, D), 1.0)
    positions = jnp.broadcast_to(jnp.arange(SEQ, dtype=jnp.int32)[None, :], (BATCH, SEQ))
    w_in = nrm(ks[2], (L, D, IN_COLS), D ** -0.5)
    conv_w = nrm(ks[3], (L, SSD_CONV, SSD_CONV_DIM), SSD_CONV ** -0.5)
    conv_b = nrm(ks[4], (L, SSD_CONV_DIM), 0.02)
    u = jax.random.uniform(ks[5], (L, SSD_HEADS), jnp.float32)
    dt0 = jnp.exp(u * (math.log(DT_MAX) - math.log(DT_MIN)) + math.log(DT_MIN))
    dt_bias = dt0 + jnp.log(-jnp.expm1(-dt0))
    a_log = jnp.log(jax.random.uniform(ks[6], (L, SSD_HEADS), jnp.float32, minval=1.0, maxval=16.0))
    d_skip = 1.0 + nrm(ks[7], (L, SSD_HEADS), 0.1)
    ssd_norm_w = 1.0 + nrm(ks[8], (L, SSD_INNER), 0.1)
    w_branch_ssd = nrm(ks[9], (L, SSD_INNER, D), DEEPNORM_BETA * SSD_INNER ** -0.5)
    w_branch_attn = nrm(ks[10], (L, ATTN_WIDTH, D), DEEPNORM_BETA * ATTN_WIDTH ** -0.5)
    w_out = nrm(ks[11], (L, D, D), DEEPNORM_BETA * D ** -0.5)
    w_ada_mix = nrm(ks[12], (L, D, 3 * D), 0.2 * D ** -0.5)
    b_ada_mix = nrm(ks[13], (L, 3 * D), 0.02)
    ln_mix_g = 1.0 + nrm(ks[14], (L, D), 0.1)
    ln_mix_b = nrm(ks[15], (L, D), 0.02)
    w_ada_ffn = nrm(ks[16], (L, D, 3 * D), 0.2 * D ** -0.5)
    b_ada_ffn = nrm(ks[17], (L, 3 * D), 0.02)
    w_router_group = nrm(ks[18], (L, D, MOE_GROUPS), D ** -0.5)
    b_router_group = nrm(ks[19], (L, MOE_GROUPS), 0.01)
    w_router_expert = nrm(ks[20], (L, D, N_EXPERTS), D ** -0.5)
    b_router_expert = nrm(ks[21], (L, N_EXPERTS), 0.01)
    w_expert_gate = nrm(ks[22], (L, N_EXPERTS, D, EXPERT_FF), D ** -0.5)
    w_expert_up = nrm(ks[23], (L, N_EXPERTS, D, EXPERT_FF), D ** -0.5)
    w_expert_down = nrm(ks[24], (L, N_EXPERTS, EXPERT_FF, D), DEEPNORM_BETA * EXPERT_FF ** -0.5)
    ln_ffn_g = 1.0 + nrm(ks[25], (L, D), 0.1)
    ln_ffn_b = nrm(ks[26], (L, D), 0.02)
    return {"x": x, "c": c, "positions": positions, "w_in": w_in, "conv_w": conv_w,
            "conv_b": conv_b, "dt_bias": dt_bias, "a_log": a_log, "d_skip": d_skip,
            "ssd_norm_w": ssd_norm_w, "w_branch_ssd": w_branch_ssd, "w_branch_attn": w_branch_attn,
            "w_out": w_out, "w_ada_mix": w_ada_mix, "b_ada_mix": b_ada_mix, "ln_mix_g": ln_mix_g,
            "ln_mix_b": ln_mix_b, "w_ada_ffn": w_ada_ffn, "b_ada_ffn": b_ada_ffn,
            "w_router_group": w_router_group, "b_router_group": b_router_group,
            "w_router_expert": w_router_expert, "b_router_expert": b_router_expert,
            "w_expert_gate": w_expert_gate, "w_expert_up": w_expert_up, "w_expert_down": w_expert_down,
            "ln_ffn_g": ln_ffn_g, "ln_ffn_b": ln_ffn_b}


def reference(x, c, positions, w_in, conv_w, conv_b, dt_bias, a_log, d_skip, ssd_norm_w,
              w_branch_ssd, w_branch_attn, w_out, w_ada_mix, b_ada_mix, ln_mix_g, ln_mix_b,
              w_ada_ffn, b_ada_ffn, w_router_group, b_router_group, w_router_expert,
              b_router_expert, w_expert_gate, w_expert_up, w_expert_down, ln_ffn_g, ln_ffn_b):
    cos, sin = rope_tables(positions)
    for l in range(DEPTH):
        shift, scale, gate = adaln(c, w_ada_mix[l], b_ada_mix[l])
        h = x * (1.0 + scale) + shift
        y = hybrid_mixer(h, cos, sin, w_in[l], conv_w[l], conv_b[l], dt_bias[l], a_log[l],
                         d_skip[l], ssd_norm_w[l], w_branch_ssd[l], w_branch_attn[l], w_out[l])
        x = layer_norm(DEEPNORM_ALPHA * x + (1.0 + gate) * y, ln_mix_g[l], ln_mix_b[l])
        shift, scale, gate = adaln(c, w_ada_ffn[l], b_ada_ffn[l])
        h = x * (1.0 + scale) + shift
        y = hierarchical_moe(h, w_router_group[l], b_router_group[l], w_router_expert[l],
                             b_router_expert[l], w_expert_gate[l], w_expert_up[l], w_expert_down[l])
        x = layer_norm(DEEPNORM_ALPHA * x + (1.0 + gate) * y, ln_ffn_g[l], ln_ffn_b[l])
    return x
```

```python
import functools
import math

import jax
import jax.numpy as jnp
from jax import lax
from jax.experimental import pallas as pl
from jax.experimental.pallas import tpu as pltpu

F32 = jnp.float32
BF16 = jnp.bfloat16
I32 = jnp.int32

D_MODEL = 1024
DEPTH = 4
SSD_INNER = 2048
SSD_HEADDIM = 64
SSD_HEADS = 32
SSD_GROUPS = 4
SSD_HPG = 8
SSD_STATE = 128
SSD_CONV = 4
SSD_CHUNK = 256
GROUP_W = SSD_INNER // SSD_GROUPS
ATTN_HEADS = 16
ATTN_HEAD_DIM = 64
ATTN_WIDTH = 1024
MOBA_BLOCK = 256
MOBA_TOPK = 3
ROPE_THETA = 10000.0
MOE_GROUPS = 4
MOE_EPG = 8
N_EXPERTS = 32
EXPERT_FF = 512
DEEPNORM_ALPHA = (2 * DEPTH) ** 0.25
LN_EPS = 1e-5
NEG_BIG = -1e30

LANE = 128
EXPERT_LANE0 = 32
ROW_BLOCK = 256

P1_Z = 0
P1_XBC = SSD_INNER
P1_GATES = P1_XBC + SSD_INNER + 2 * SSD_GROUPS * SSD_STATE
P1_DT = P1_GATES + 2 * D_MODEL
P1_COLS = P1_DT + SSD_GROUPS * LANE

VMEM_LIMIT = 56 * 1024 * 1024


def _cparams(sem):
    return pltpu.CompilerParams(dimension_semantics=sem, vmem_limit_bytes=VMEM_LIMIT)


def _split3(v):
    hi = v.astype(BF16)
    r1 = v - hi.astype(F32)
    mid = r1.astype(BF16)
    lo = (r1 - mid.astype(F32)).astype(BF16)
    return hi, mid, lo


def _dot(a, b):
    return jnp.dot(a, b, preferred_element_type=F32)


def _dot_nt(a, b):
    return lax.dot_general(a, b, (((1,), (1,)), ((), ())), preferred_element_type=F32)


def _silu(v):
    return v * jax.nn.sigmoid(v)


def _layer_norm(r, g, b):
    mu = jnp.mean(r, axis=-1, keepdims=True)
    d = r - mu
    var = jnp.mean(d * d, axis=-1, keepdims=True)
    return d * lax.rsqrt(var + LN_EPS) * g + b


def _adaln_kernel(c_ref, w_ref, b_ref, o_ref):
    s = _silu(c_ref[...]).astype(BF16)
    o_ref[0] = _dot(s, w_ref[0].astype(BF16)) + b_ref[0]


def adaln_all(c8, w, b):
    nl = w.shape[0]
    tn = 1024
    return pl.pallas_call(
        _adaln_kernel,
        grid=(nl, 3 * D_MODEL // tn),
        in_specs=[pl.BlockSpec((8, D_MODEL), lambda n, j: (0, 0)),
                  pl.BlockSpec((1, D_MODEL, tn), lambda n, j: (n, 0, j)),
                  pl.BlockSpec((1, 1, tn), lambda n, j: (n, 0, j))],
        out_specs=pl.BlockSpec((1, 8, tn), lambda n, j: (n, 0, j)),
        out_shape=jax.ShapeDtypeStruct((nl, 8, 3 * D_MODEL), F32),
        compiler_params=_cparams(("parallel", "parallel")),
        name="adaln",
    )(c8, w, b.reshape(nl, 1, 3 * D_MODEL))


def _rope_kernel(pos_ref, invf_ref, cos_ref, sin_ref):
    ang = pos_ref[...].astype(F32) * invf_ref[...]
    lane = lax.broadcasted_iota(I32, ang.shape, 1)
    sign = jnp.where((lane & 63) < 32, -1.0, 1.0)
    cos_ref[...] = jnp.cos(ang)
    sin_ref[...] = jnp.sin(ang) * sign


def rope_tables(pos_col, invf):
    t = pos_col.shape[0]
    tm = min(t, 2048)
    return pl.pallas_call(
        _rope_kernel,
        grid=(t // tm,),
        in_specs=[pl.BlockSpec((tm, 1), lambda i: (i, 0)),
                  pl.BlockSpec((1, LANE), lambda i: (0, 0))],
        out_specs=[pl.BlockSpec((tm, LANE), lambda i: (i, 0))] * 2,
        out_shape=[jax.ShapeDtypeStruct((t, LANE), F32)] * 2,
        compiler_params=_cparams(("parallel",)),
        name="rope_tables",
    )(pos_col, invf)


def _inproj_kernel(n_rope, x_ref, mod_ref, w_ref, *rest):
    if n_rope:
        cos_ref, sin_ref, o_ref, h_scr = rest
    else:
        o_ref, h_scr = rest
    j = pl.program_id(1)

    @pl.when(j == 0)
    def _():
        shift = mod_ref[0, :, 0:D_MODEL]
        scale = mod_ref[0, :, D_MODEL:2 * D_MODEL]
        h_scr[...] = (x_ref[...] * (1.0 + scale) + shift).astype(BF16)

    acc = _dot(h_scr[...], w_ref[...])
    if not n_rope:
        o_ref[...] = acc.astype(o_ref.dtype)
        return

    @pl.when(j < n_rope)
    def _():
        tn = acc.shape[1]
        reps = tn // LANE
        cos = jnp.concatenate([cos_ref[...]] * reps, axis=1)
        sin = jnp.concatenate([sin_ref[...]] * reps, axis=1)
        lane = lax.broadcasted_iota(I32, acc.shape, 1)
        swapped = jnp.where((lane & 63) < 32, pltpu.roll(acc, tn - 32, 1), pltpu.roll(acc, 32, 1))
        o_ref[...] = (acc * cos + swapped * sin).astype(o_ref.dtype)

    @pl.when(j >= n_rope)
    def _():
        o_ref[...] = acc.astype(o_ref.dtype)


def inproj(x, mod, w, seq, out_dtype, rope=None, n_rope_cols=0):
    t = x.shape[0]
    n = w.shape[1]
    tm = min(1024, seq)
    tn = 512
    n_rope = n_rope_cols // tn
    in_specs = [pl.BlockSpec((tm, D_MODEL), lambda i, j: (i, 0)),
                pl.BlockSpec((1, 1, 3 * D_MODEL), lambda i, j: ((i * tm) // seq, 0, 0)),
                pl.BlockSpec((D_MODEL, tn), lambda i, j: (0, j))]
    args = [x, mod, w]
    if n_rope:
        in_specs += [pl.BlockSpec((tm, LANE), lambda i, j: (i, 0))] * 2
        args += list(rope)
    return pl.pallas_call(
        functools.partial(_inproj_kernel, n_rope),
        grid=(t // tm, n // tn),
        in_specs=in_specs,
        out_specs=pl.BlockSpec((tm, tn), lambda i, j: (i, j)),
        out_shape=jax.ShapeDtypeStruct((t, n), out_dtype),
        scratch_shapes=[pltpu.VMEM((tm, D_MODEL), BF16)],
        compiler_params=_cparams(("parallel", "arbitrary")),
        name="inproj_rope" if n_rope else "inproj",
    )(*args)


def _ssd_kernel(z_ref, x_ref, b_ref, c_ref, dt_ref, cwx_ref, cwb_ref, cwc_ref, cbx_ref, cbb_ref, cbc_ref,
                dtb_ref, alog_ref, dsk_ref, nw_ref, tri_ref, exp_ref, o_ref,
                ex_scr, eb_scr, ec_scr, st_scr):
    L = SSD_CHUNK
    ci = pl.program_id(2)

    @pl.when(ci == 0)
    def _():
        ex_scr[0:8, :] = jnp.zeros((8, GROUP_W), F32)
        eb_scr[0:8, :] = jnp.zeros((8, SSD_STATE), F32)
        ec_scr[0:8, :] = jnp.zeros((8, SSD_STATE), F32)
        st_scr[...] = jnp.zeros_like(st_scr)

    def conv_silu(raw_ref, scr, w_ref, bias_ref):
        scr[8:8 + L, :] = raw_ref[...]
        acc = bias_ref[...] + w_ref[SSD_CONV - 1:SSD_CONV, :] * scr[8:8 + L, :]
        for k in range(SSD_CONV - 1):
            acc = acc + w_ref[k:k + 1, :] * scr[5 + k:5 + k + L, :]
        scr[0:8, :] = scr[L:L + 8, :]
        return _silu(acc)

    xs = conv_silu(x_ref, ex_scr, cwx_ref, cbx_ref)
    bs = conv_silu(b_ref, eb_scr, cwb_ref, cbb_ref)
    cc = conv_silu(c_ref, ec_scr, cwc_ref, cbc_ref)

    dtr = dt_ref[...] + dtb_ref[0]
    dtv = jnp.maximum(dtr, 0.0) + jnp.log1p(jnp.exp(-jnp.abs(dtr)))
    a_neg = -jnp.exp(alog_ref[0])
    dta = dtv * a_neg
    tri = tri_ref[...]
    h3 = _split3(dta)
    cs = _dot(tri, h3[0]) + _dot(tri, h3[1]) + _dot(tri, h3[2])
    cs_t = cs.T
    cs_last = cs[L - 1:L, :]
    ecs = jnp.exp(cs)
    dte = dtv * jnp.exp(cs_last - cs)
    dlast = jnp.broadcast_to(jnp.exp(cs_last), (8, LANE))

    expand = exp_ref[...]

    def widen(v):
        p = _split3(v)
        return _dot(p[0], expand) + _dot(p[1], expand) + _dot(p[2], expand)

    dt_x = widen(dtv)
    ecs_x = widen(ecs)
    dte_x = widen(dte)
    dlast_x = widen(dlast)[0:1, :]

    xdt = xs * dt_x
    xdt_b = xdt.astype(BF16)
    bs_b = bs.astype(BF16)
    cc_b = cc.astype(BF16)
    cb = _dot_nt(cc_b, bs_b)
    row = lax.broadcasted_iota(I32, (L, L), 0)
    col = lax.broadcasted_iota(I32, (L, L), 1)
    causal = row >= col
    lane = lax.broadcasted_iota(I32, (L, LANE), 1)
    first_head = lane < SSD_HEADDIM

    parts = []
    for p in range(SSD_HPG // 2):
        xp = xdt_b[:, LANE * p:LANE * (p + 1)]
        ys = []
        for hh in range(2):
            i = 2 * p + hh
            seg = cs[:, i:i + 1] - cs_t[i:i + 1, :]
            dec = jnp.exp(jnp.where(causal, seg, -jnp.inf))
            ys.append(_dot((cb * dec).astype(BF16), xp))
        parts.append(jnp.where(first_head, ys[0], ys[1]))
    y = jnp.concatenate(parts, axis=1)

    st = st_scr[...]
    y = y + _dot(cc_b, st.astype(BF16)) * ecs_x
    xw = (xs * dte_x).astype(BF16)
    st_scr[...] = st * dlast_x + _dot(bs.T.astype(BF16), xw)

    y = y + dsk_ref[...] * xs
    gy = y * _silu(z_ref[...])
    ms = jnp.mean(gy * gy, axis=-1, keepdims=True)
    o_ref[...] = (gy * lax.rsqrt(ms + LN_EPS) * nw_ref[...]).astype(o_ref.dtype)


def ssd_mixer(p1, conv_w, conv_b, dtb, alog, dsk, nw, tri, expand, batch, seq):
    t = p1.shape[0]
    L = SSD_CHUNK
    nc = seq // L
    xb0 = P1_XBC // GROUP_W
    bb0 = (P1_XBC + SSD_INNER) // SSD_STATE
    cb0 = bb0 + SSD_GROUPS
    db0 = P1_DT // LANE
    wb0 = SSD_INNER // SSD_STATE
    rowmap = lambda b, g, c: b * nc + c
    in_specs = [
        pl.BlockSpec((L, GROUP_W), lambda b, g, c: (rowmap(b, g, c), g)),
        pl.BlockSpec((L, GROUP_W), lambda b, g, c: (rowmap(b, g, c), xb0 + g)),
        pl.BlockSpec((L, SSD_STATE), lambda b, g, c: (rowmap(b, g, c), bb0 + g)),
        pl.BlockSpec((L, SSD_STATE), lambda b, g, c: (rowmap(b, g, c), cb0 + g)),
        pl.BlockSpec((L, LANE), lambda b, g, c: (rowmap(b, g, c), db0 + g)),
        pl.BlockSpec((SSD_CONV, GROUP_W), lambda b, g, c: (0, g)),
        pl.BlockSpec((SSD_CONV, SSD_STATE), lambda b, g, c: (0, wb0 + g)),
        pl.BlockSpec((SSD_CONV, SSD_STATE), lambda b, g, c: (0, wb0 + SSD_GROUPS + g)),
        pl.BlockSpec((1, GROUP_W), lambda b, g, c: (0, g)),
        pl.BlockSpec((1, SSD_STATE), lambda b, g, c: (0, wb0 + g)),
        pl.BlockSpec((1, SSD_STATE), lambda b, g, c: (0, wb0 + SSD_GROUPS + g)),
        pl.BlockSpec((1, 1, LANE), lambda b, g, c: (g, 0, 0)),
        pl.BlockSpec((1, 1, LANE), lambda b, g, c: (g, 0, 0)),
        pl.BlockSpec((1, GROUP_W), lambda b, g, c: (0, g)),
        pl.BlockSpec((1, GROUP_W), lambda b, g, c: (0, g)),
        pl.BlockSpec((L, L), lambda b, g, c: (0, 0)),
        pl.BlockSpec((LANE, GROUP_W), lambda b, g, c: (0, 0)),
    ]
    return pl.pallas_call(
        _ssd_kernel,
        grid=(batch, SSD_GROUPS, nc),
        in_specs=in_specs,
        out_specs=pl.BlockSpec((L, GROUP_W), lambda b, g, c: (rowmap(b, g, c), g)),
        out_shape=jax.ShapeDtypeStruct((t, SSD_INNER), BF16),
        scratch_shapes=[pltpu.VMEM((L + 8, GROUP_W), F32), pltpu.VMEM((L + 8, SSD_STATE), F32),
                        pltpu.VMEM((L + 8, SSD_STATE), F32), pltpu.VMEM((SSD_STATE, GROUP_W), F32)],
        compiler_params=_cparams(("parallel", "parallel", "arbitrary")),
        name="ssd",
    )(p1, p1, p1, p1, p1, conv_w, conv_w, conv_w, conv_b, conv_b, conv_b, dtb, alog, dsk, nw, tri, expand)


def _moba_kernel(nb, q_ref, k_ref, v_ref, y_ref, o_ref, km_scr):
    BS = MOBA_BLOCK
    qi = pl.program_id(2)

    @pl.when(qi == 0)
    def _():
        km_scr[...] = jnp.zeros_like(km_scr)

        def body(j, carry):
            blk = k_ref[pl.ds(pl.multiple_of(j * BS, BS), BS), :].astype(F32)
            km_scr[pl.ds(j, 1), :] = jnp.mean(blk, axis=0, keepdims=True)
            return carry

        lax.fori_loop(0, nb, body, 0)

    q = q_ref[...].astype(F32)
    lane = lax.broadcasted_iota(I32, (BS, LANE), 1)
    lane_f = lane.astype(F32)
    is_a = lane < ATTN_HEAD_DIM
    q_heads = (jnp.where(is_a, q, 0.0).astype(BF16), jnp.where(is_a, 0.0, q).astype(BF16))
    kmb = km_scr[...].astype(BF16)

    def neg_mask(qh):
        g = _dot_nt(qh, kmb)
        g = jnp.where(lane < qi, g, -jnp.inf)
        sel = jnp.zeros((BS, LANE), F32)
        for _ in range(MOBA_TOPK):
            m = jnp.max(g, axis=1, keepdims=True)
            hit = jnp.logical_and(g == m, m > -jnp.inf)
            idx = jnp.min(jnp.where(hit, lane_f, float(LANE)), axis=1, keepdims=True)
            pick = lane_f == idx
            sel = jnp.where(pick, 1.0, sel)
            g = jnp.where(pick, -jnp.inf, g)
        return ((sel - 1.0) * (-NEG_BIG)).astype(BF16)

    nmasks = (neg_mask(q_heads[0]), neg_mask(q_heads[1]))

    own0 = pl.multiple_of(qi * BS, BS)
    k_own = k_ref[pl.ds(own0, BS), :]
    v_own = v_ref[pl.ds(own0, BS), :]
    row = lax.broadcasted_iota(I32, (BS, BS), 0)
    col = lax.broadcasted_iota(I32, (BS, BS), 1)
    causal = row >= col

    def first(qh):
        s = jnp.where(causal, _dot_nt(qh, k_own), NEG_BIG)
        m = jnp.max(s, axis=1, keepdims=True)
        p = jnp.exp(s - m)
        return m, jnp.sum(p, axis=1, keepdims=True), _dot(p.astype(BF16), v_own)

    m_a, l_a, pv_a = first(q_heads[0])
    m_b, l_b, pv_b = first(q_heads[1])
    acc = jnp.where(is_a, pv_a, pv_b)

    def body(j, carry):
        m_a, l_a, m_b, l_b, acc = carry
        j0 = pl.multiple_of(j * BS, BS)
        kj = k_ref[pl.ds(j0, BS), :]
        vj = v_ref[pl.ds(j0, BS), :]
        yj = y_ref[j]

        def upd(qh, nm, m, l):
            s = _dot_nt(qh, kj) + _dot(nm, yj)
            m2 = jnp.maximum(m, jnp.max(s, axis=1, keepdims=True))
            a = jnp.exp(m - m2)
            p = jnp.exp(s - m2)
            return m2, a * l + jnp.sum(p, axis=1, keepdims=True), a, _dot(p.astype(BF16), vj)

        m_a, l_a, a_a, pv_a = upd(q_heads[0], nmasks[0], m_a, l_a)
        m_b, l_b, a_b, pv_b = upd(q_heads[1], nmasks[1], m_b, l_b)
        acc = jnp.where(is_a, a_a, a_b) * acc + jnp.where(is_a, pv_a, pv_b)
        return m_a, l_a, m_b, l_b, acc

    m_a, l_a, m_b, l_b, acc = lax.fori_loop(0, qi, body, (m_a, l_a, m_b, l_b, acc))
    o_ref[...] = (acc / jnp.where(is_a, l_a, l_b)).astype(o_ref.dtype)


def moba_attention(qkv, blocksel, batch, seq):
    t = qkv.shape[0]
    BS = MOBA_BLOCK
    nb = seq // BS
    hp = ATTN_HEADS // 2
    return pl.pallas_call(
        functools.partial(_moba_kernel, nb),
        grid=(batch, hp, nb),
        in_specs=[pl.BlockSpec((BS, LANE), lambda b, h, i: (b * nb + i, h)),
                  pl.BlockSpec((seq, LANE), lambda b, h, i: (b, hp + h)),
                  pl.BlockSpec((seq, LANE), lambda b, h, i: (b, 2 * hp + h)),
                  pl.BlockSpec((nb, LANE, BS), lambda b, h, i: (0, 0, 0))],
        out_specs=pl.BlockSpec((BS, LANE), lambda b, h, i: (b * nb + i, h)),
        out_shape=jax.ShapeDtypeStruct((t, ATTN_WIDTH), BF16),
        scratch_shapes=[pltpu.VMEM((LANE, LANE), F32)],
        compiler_params=_cparams(("parallel", "parallel", "arbitrary")),
        name="moba",
    )(qkv, qkv, qkv, blocksel)


def _mixout_kernel(ys_ref, ya_ref, gs_ref, ga_ref, x_ref, mod_ref, wbs_ref, wba_ref, wo_ref, g_ref, b_ref, o_ref):
    a = _dot(ys_ref[...], wbs_ref[...])
    bb = _dot(ya_ref[...], wba_ref[...])
    merged = jax.nn.sigmoid(gs_ref[...]) * a + jax.nn.sigmoid(ga_ref[...]) * bb
    y = _dot(merged.astype(BF16), wo_ref[...])
    gate = mod_ref[0, :, 2 * D_MODEL:3 * D_MODEL]
    r = DEEPNORM_ALPHA * x_ref[...] + (1.0 + gate) * y
    o_ref[...] = _layer_norm(r, g_ref[...], b_ref[...])


def mix_out(y_ssd, y_attn, p1, x, mod, wbs, wba, wo, ln_g, ln_b, seq):
    t = x.shape[0]
    tm = min(512, seq)
    gb = P1_GATES // D_MODEL
    full = lambda shape: pl.BlockSpec(shape, lambda i: (0, 0))
    return pl.pallas_call(
        _mixout_kernel,
        grid=(t // tm,),
        in_specs=[pl.BlockSpec((tm, SSD_INNER), lambda i: (i, 0)),
                  pl.BlockSpec((tm, ATTN_WIDTH), lambda i: (i, 0)),
                  pl.BlockSpec((tm, D_MODEL), lambda i: (i, gb)),
                  pl.BlockSpec((tm, D_MODEL), lambda i: (i, gb + 1)),
                  pl.BlockSpec((tm, D_MODEL), lambda i: (i, 0)),
                  pl.BlockSpec((1, 1, 3 * D_MODEL), lambda i: ((i * tm) // seq, 0, 0)),
                  full((SSD_INNER, D_MODEL)), full((ATTN_WIDTH, D_MODEL)), full((D_MODEL, D_MODEL)),
                  full((1, D_MODEL)), full((1, D_MODEL))],
        out_specs=pl.BlockSpec((tm, D_MODEL), lambda i: (i, 0)),
        out_shape=jax.ShapeDtypeStruct((t, D_MODEL), F32),
        compiler_params=_cparams(("parallel",)),
        name="mix_out",
    )(y_ssd, y_attn, p1, p1, x, mod, wbs, wba, wo, ln_g, ln_b)


def _router_kernel(x_ref, mod_ref, wr_ref, br_ref, tri_ref, h_ref, ri_ref, rw_ref, cnt_ref, base_scr):
    i = pl.program_id(0)

    @pl.when(i == 0)
    def _():
        base_scr[...] = jnp.zeros_like(base_scr)

    shift = mod_ref[0, :, 0:D_MODEL]
    scale = mod_ref[0, :, D_MODEL:2 * D_MODEL]
    h = x_ref[...] * (1.0 + scale) + shift
    h_ref[...] = h
    logits = _dot(h.astype(BF16), wr_ref[...]) + br_ref[...]
    tm = logits.shape[0]
    lane_i = lax.broadcasted_iota(I32, (tm, LANE), 1)
    lane = lane_i.astype(F32)
    neg = -jnp.inf

    def first_max(v):
        m = jnp.max(v, axis=1, keepdims=True)
        idx = jnp.min(jnp.where(v == m, lane, float(LANE)), axis=1, keepdims=True)
        return m, idx

    lg = jnp.where(lane < MOE_GROUPS, logits, neg)
    ex = jnp.exp(lg - jnp.max(lg, axis=1, keepdims=True))
    prob = ex / jnp.sum(ex, axis=1, keepdims=True)
    g_val, g_idx = first_max(jnp.where(lane < MOE_GROUPS, prob, neg))
    lo = EXPERT_LANE0 + MOE_EPG * g_idx
    el = jnp.where(jnp.logical_and(lane >= lo, lane < lo + MOE_EPG), logits, neg)
    v1, i1 = first_max(el)
    v2, i2 = first_max(jnp.where(lane == i1, neg, el))
    e2 = jnp.exp(v2 - v1)
    den = 1.0 + e2
    w1 = (1.0 / den) * g_val
    w2 = (e2 / den) * g_val
    e_1 = i1 - EXPERT_LANE0
    e_2 = i2 - EXPERT_LANE0

    onehot = jnp.where(lane == e_1, 1.0, 0.0) + jnp.where(lane == e_2, 1.0, 0.0)
    before = _dot(tri_ref[...], onehot.astype(BF16)) + base_scr[0:1, :]
    r1 = jnp.sum(jnp.where(lane == e_1, before, 0.0), axis=1, keepdims=True)
    r2 = jnp.sum(jnp.where(lane == e_2, before, 0.0), axis=1, keepdims=True)
    base_scr[...] = base_scr[...] + jnp.sum(onehot, axis=0, keepdims=True)
    cnt_ref[...] = base_scr[...].astype(I32)

    ri = jnp.where(lane == 0, e_1, jnp.where(lane == 1, e_2, jnp.where(lane == 2, r1, jnp.where(lane == 3, r2, 0.0))))
    ri_ref[...] = ri.astype(I32)
    rw_ref[...] = jnp.where(lane == 0, w1, jnp.where(lane == 1, w2, 0.0))


def moe_router(x, mod, wr, br, tri, seq):
    t = x.shape[0]
    tm = tri.shape[0]
    return pl.pallas_call(
        _router_kernel,
        grid=(t // tm,),
        in_specs=[pl.BlockSpec((tm, D_MODEL), lambda i: (i, 0)),
                  pl.BlockSpec((1, 1, 3 * D_MODEL), lambda i: ((i * tm) // seq, 0, 0)),
                  pl.BlockSpec((D_MODEL, LANE), lambda i: (0, 0)),
                  pl.BlockSpec((1, LANE), lambda i: (0, 0)),
                  pl.BlockSpec((tm, tm), lambda i: (0, 0))],
        out_specs=[pl.BlockSpec((tm, D_MODEL), lambda i: (i, 0)),
                   pl.BlockSpec((tm, LANE), lambda i: (i, 0)),
                   pl.BlockSpec((tm, LANE), lambda i: (i, 0)),
                   pl.BlockSpec((8, LANE), lambda i: (0, 0))],
        out_shape=[jax.ShapeDtypeStruct((t, D_MODEL), F32),
                   jax.ShapeDtypeStruct((t, LANE), I32),
                   jax.ShapeDtypeStruct((t, LANE), F32),
                   jax.ShapeDtypeStruct((8, LANE), I32)],
        scratch_shapes=[pltpu.VMEM((8, LANE), F32)],
        compiler_params=_cparams(("arbitrary",)),
        name="moe_router",
    )(x, mod, wr, br, tri)


DISPATCH_TOKENS = 512


def _dispatch_kernel(pos_ref, h_hbm, xr_in, xr_hbm, sem):
    del xr_in
    i = pl.program_id(0)
    tm = DISPATCH_TOKENS

    def row_copy(tok, dst):
        return pltpu.make_async_copy(h_hbm.at[pl.ds(tok, 1)], xr_hbm.at[pl.ds(dst, 1)], sem)

    def start(t, carry):
        tok = i * tm + t
        row_copy(tok, pos_ref[0, 0, 2 * t]).start()
        row_copy(tok, pos_ref[0, 0, 2 * t + 1]).start()
        return carry

    lax.fori_loop(0, tm, start, 0, unroll=8)

    def wait(t, carry):
        row_copy(0, 0).wait()
        row_copy(0, 0).wait()
        return carry

    lax.fori_loop(0, tm, wait, 0, unroll=8)


def moe_dispatch(h, pos, n_rows):
    t = h.shape[0]
    tm = DISPATCH_TOKENS
    zeros = jnp.zeros((n_rows, D_MODEL), F32)
    return pl.pallas_call(
        _dispatch_kernel,
        grid=(t // tm,),
        in_specs=[pl.BlockSpec((1, 1, 2 * tm), lambda i: (i, 0, 0), memory_space=pltpu.SMEM),
                  pl.BlockSpec(memory_space=pl.ANY),
                  pl.BlockSpec(memory_space=pl.ANY)],
        out_specs=pl.BlockSpec(memory_space=pl.ANY),
        out_shape=jax.ShapeDtypeStruct((n_rows, D_MODEL), F32),
        scratch_shapes=[pltpu.SemaphoreType.DMA(())],
        input_output_aliases={2: 0},
        compiler_params=_cparams(("arbitrary",)),
        name="moe_dispatch",
    )(pos.reshape(t // tm, 1, 2 * tm), h, zeros)


def _expert_kernel(be_ref, nu_ref, x_ref, wg_ref, wu_ref, wd_ref, o_ref):
    i = pl.program_id(0)

    @pl.when(i < nu_ref[0])
    def _():
        xb = x_ref[...].astype(BF16)
        hmid = _silu(_dot(xb, wg_ref[0])) * _dot(xb, wu_ref[0])
        o_ref[...] = _dot(hmid.astype(BF16), wd_ref[0])

    @pl.when(i >= nu_ref[0])
    def _():
        o_ref[...] = jnp.zeros_like(o_ref)


def moe_experts(x_rows, block_e, n_used, wg, wu, wd):
    n_rows = x_rows.shape[0]
    nblk = n_rows // ROW_BLOCK
    rowmap = lambda i, be, nu: (jnp.minimum(i, nu[0] - 1), 0)
    grid_spec = pltpu.PrefetchScalarGridSpec(
        num_scalar_prefetch=2,
        grid=(nblk,),
        in_specs=[pl.BlockSpec((ROW_BLOCK, D_MODEL), rowmap),
                  pl.BlockSpec((1, D_MODEL, EXPERT_FF), lambda i, be, nu: (be[i], 0, 0)),
                  pl.BlockSpec((1, D_MODEL, EXPERT_FF), lambda i, be, nu: (be[i], 0, 0)),
                  pl.BlockSpec((1, EXPERT_FF, D_MODEL), lambda i, be, nu: (be[i], 0, 0))],
        out_specs=pl.BlockSpec((ROW_BLOCK, D_MODEL), lambda i, be, nu: (i, 0)),
    )
    return pl.pallas_call(
        _expert_kernel,
        grid_spec=grid_spec,
        out_shape=jax.ShapeDtypeStruct((n_rows, D_MODEL), F32),
        compiler_params=_cparams(("arbitrary",)),
        name="moe_experts",
    )(block_e, n_used, x_rows, wg, wu, wd)


COMBINE_TOKENS = 256


def _combine_kernel(pos_ref, yr_hbm, rw_ref, x_ref, mod_ref, g_ref, b_ref, o_ref, buf0, buf1, sem):
    tm = COMBINE_TOKENS

    def row_copy(src, buf, t):
        return pltpu.make_async_copy(yr_hbm.at[pl.ds(src, 1)], buf.at[pl.ds(t, 1)], sem)

    def start(t, carry):
        row_copy(pos_ref[0, 0, 2 * t], buf0, t).start()
        row_copy(pos_ref[0, 0, 2 * t + 1], buf1, t).start()
        return carry

    lax.fori_loop(0, tm, start, 0, unroll=8)

    def wait(t, carry):
        row_copy(0, buf0, 0).wait()
        row_copy(0, buf1, 0).wait()
        return carry

    lax.fori_loop(0, tm, wait, 0, unroll=8)

    w = rw_ref[...]
    y = w[:, 0:1] * buf0[...] + w[:, 1:2] * buf1[...]
    gate = mod_ref[0, :, 2 * D_MODEL:3 * D_MODEL]
    r = DEEPNORM_ALPHA * x_ref[...] + (1.0 + gate) * y
    o_ref[...] = _layer_norm(r, g_ref[...], b_ref[...])


def moe_combine(y_rows, pos, rw, x, mod, ln_g, ln_b, seq):
    t = x.shape[0]
    tm = COMBINE_TOKENS
    return pl.pallas_call(
        _combine_kernel,
        grid=(t // tm,),
        in_specs=[pl.BlockSpec((1, 1, 2 * tm), lambda i: (i, 0, 0), memory_space=pltpu.SMEM),
                  pl.BlockSpec(memory_space=pl.ANY),
                  pl.BlockSpec((tm, LANE), lambda i: (i, 0)),
                  pl.BlockSpec((tm, D_MODEL), lambda i: (i, 0)),
                  pl.BlockSpec((1, 1, 3 * D_MODEL), lambda i: ((i * tm) // seq, 0, 0)),
                  pl.BlockSpec((1, D_MODEL), lambda i: (0, 0)),
                  pl.BlockSpec((1, D_MODEL), lambda i: (0, 0))],
        out_specs=pl.BlockSpec((tm, D_MODEL), lambda i: (i, 0)),
        out_shape=jax.ShapeDtypeStruct((t, D_MODEL), F32),
        scratch_shapes=[pltpu.VMEM((tm, D_MODEL), F32), pltpu.VMEM((tm, D_MODEL), F32),
                        pltpu.SemaphoreType.DMA(())],
        compiler_params=_cparams(("arbitrary",)),
        name="moe_combine",
    )(pos.reshape(t // tm, 1, 2 * tm), y_rows, rw, x, mod, ln_g, ln_b)


def moe_sublayer(x, mod, wr, br, tri, wg, wu, wd, ln_g, ln_b, seq):
    t = x.shape[0]
    h, ri, rw, cnt = moe_router(x, mod, wr, br, tri, seq)
    counts = cnt[0, :N_EXPERTS]
    padded = (counts + ROW_BLOCK - 1) // ROW_BLOCK * ROW_BLOCK
    ends = jnp.cumsum(padded)
    pstarts = ends - padded
    n_rows = t * 2 + N_EXPERTS * ROW_BLOCK
    nblk = n_rows // ROW_BLOCK
    blk0 = jnp.arange(nblk, dtype=I32) * ROW_BLOCK
    block_e = jnp.minimum(jnp.sum((ends[None, :] <= blk0[:, None]).astype(I32), axis=1), N_EXPERTS - 1)
    n_used = (ends[-1] // ROW_BLOCK).astype(I32).reshape(1)
    eid = ri[:, 0:2]
    rank = ri[:, 2:4]
    start_of = jnp.sum(jnp.where(eid[:, :, None] == jnp.arange(N_EXPERTS, dtype=I32), pstarts, 0), axis=-1)
    pos = (start_of + rank).reshape(t * 2).astype(I32)
    x_rows = moe_dispatch(h, pos, n_rows)
    y_rows = moe_experts(x_rows, block_e.astype(I32), n_used, wg, wu, wd)
    return moe_combine(y_rows, pos, rw, x, mod, ln_g, ln_b, seq)


def kernel(x, c, positions, w_in, conv_w, conv_b, dt_bias, a_log, d_skip, ssd_norm_w, w_branch_ssd, w_branch_attn, w_out, w_ada_mix, b_ada_mix, ln_mix_g, ln_mix_b, w_ada_ffn, b_ada_ffn, w_router_group, b_router_group, w_router_expert, b_router_expert, w_expert_gate, w_expert_up, w_expert_down, ln_ffn_g, ln_ffn_b):
    batch, seq, d = x.shape
    t = batch * seq
    nl = w_in.shape[0]
    xf = x.reshape(t, d)

    c8 = jnp.zeros((8, d), F32).at[:batch].set(c)
    mod_mix = adaln_all(c8, w_ada_mix, b_ada_mix)[:, :batch, None, :]
    mod_ffn = adaln_all(c8, w_ada_ffn, b_ada_ffn)[:, :batch, None, :]
    inv_freq = ROPE_THETA ** (-jnp.arange(0, ATTN_HEAD_DIM, 2, dtype=F32) / ATTN_HEAD_DIM)
    invf = jnp.tile(inv_freq, LANE // 32)[None, :]
    rope = rope_tables(positions.reshape(t, 1).astype(I32), invf)

    tri_chunk = jnp.tril(jnp.ones((SSD_CHUNK, SSD_CHUNK), F32)).astype(BF16)
    expand = (jnp.arange(LANE)[:, None] == jnp.arange(GROUP_W)[None, :] // SSD_HEADDIM).astype(BF16)
    nb = seq // MOBA_BLOCK
    blocksel = (jnp.arange(nb)[:, None, None] == jnp.arange(LANE)[None, :, None]).astype(BF16)
    blocksel = jnp.broadcast_to(blocksel, (nb, LANE, MOBA_BLOCK))
    router_tm = min(512, seq)
    tri_router = jnp.tril(jnp.ones((router_tm, router_tm), F32), k=-1).astype(BF16)

    s_z, s_xbc, s_dt, s_qkv = SSD_INNER, SSD_INNER + 3072, SSD_INNER + 3072 + SSD_HEADS, SSD_INNER + 3072 + SSD_HEADS + 3 * ATTN_WIDTH
    head_lane = jnp.arange(SSD_HEADS) % SSD_HPG + (jnp.arange(SSD_HEADS) // SSD_HPG) * LANE

    for l in range(nl):
        wl = w_in[l]
        w_dt = jnp.zeros((d, SSD_GROUPS * LANE), F32).at[:, head_lane].set(wl[:, s_xbc:s_dt])
        w_p1 = jnp.concatenate([wl[:, :s_xbc], wl[:, s_qkv:], w_dt], axis=1).astype(BF16)
        qscale = jnp.concatenate([jnp.full((ATTN_WIDTH,), ATTN_HEAD_DIM ** -0.5, F32), jnp.ones((2 * ATTN_WIDTH,), F32)])
        w_qkv = (wl[:, s_dt:s_qkv] * qscale).astype(BF16)
        dtb = jnp.zeros((SSD_GROUPS, 1, LANE), F32).at[:, 0, :SSD_HPG].set(dt_bias[l].reshape(SSD_GROUPS, SSD_HPG))
        alog = jnp.zeros((SSD_GROUPS, 1, LANE), F32).at[:, 0, :SSD_HPG].set(a_log[l].reshape(SSD_GROUPS, SSD_HPG))
        dsk = jnp.repeat(d_skip[l], SSD_HEADDIM)[None, :]

        p1 = inproj(xf, mod_mix[l], w_p1, seq, F32)
        qkv = inproj(xf, mod_mix[l], w_qkv, seq, BF16, rope=rope, n_rope_cols=2 * ATTN_WIDTH)
        y_ssd = ssd_mixer(p1, conv_w[l], conv_b[l][None, :], dtb, alog, dsk, ssd_norm_w[l][None, :],
                          tri_chunk, expand, batch, seq)
        y_attn = moba_attention(qkv, blocksel, batch, seq)
        xf = mix_out(y_ssd, y_attn, p1, xf, mod_mix[l], w_branch_ssd[l].astype(BF16), w_branch_attn[l].astype(BF16),
                     w_out[l].astype(BF16), ln_mix_g[l][None, :], ln_mix_b[l][None, :], seq)

        wr = jnp.zeros((d, LANE), F32).at[:, :MOE_GROUPS].set(w_router_group[l])
        wr = wr.at[:, EXPERT_LANE0:EXPERT_LANE0 + N_EXPERTS].set(w_router_expert[l]).astype(BF16)
        br = jnp.zeros((1, LANE), F32).at[0, :MOE_GROUPS].set(b_router_group[l])
        br = br.at[0, EXPERT_LANE0:EXPERT_LANE0 + N_EXPERTS].set(b_router_expert[l])
        xf = moe_sublayer(xf, mod_ffn[l], wr, br, tri_router, w_expert_gate[l].astype(BF16), w_expert_up[l].astype(BF16),
                          w_expert_down[l].astype(BF16), ln_ffn_g[l][None, :], ln_ffn_b[l][None, :], seq)

    return xf.reshape(batch, seq, d)
```

```python
import functools
import math

import jax
import jax.numpy as jnp
from jax import lax
from jax.experimental import pallas as pl
from jax.experimental.pallas import tpu as pltpu

F32 = jnp.float32
BF16 = jnp.bfloat16
I32 = jnp.int32

D_MODEL = 1024
DEPTH = 4
SSD_INNER = 2048
SSD_HEADDIM = 64
SSD_HEADS = 32
SSD_GROUPS = 4
SSD_HPG = 8
SSD_STATE = 128
SSD_CONV = 4
SSD_CHUNK = 256
GROUP_W = SSD_INNER // SSD_GROUPS
ATTN_HEADS = 16
ATTN_HEAD_DIM = 64
ATTN_WIDTH = 1024
MOBA_BLOCK = 256
MOBA_TOPK = 3
ROPE_THETA = 10000.0
MOE_GROUPS = 4
MOE_EPG = 8
N_EXPERTS = 32
EXPERT_FF = 512
DEEPNORM_ALPHA = (2 * DEPTH) ** 0.25
LN_EPS = 1e-5
NEG_BIG = -1e30

LANE = 128
EXPERT_LANE0 = 32
ROW_BLOCK = 256

P1_Z = 0
P1_XBC = SSD_INNER
P1_GATES = P1_XBC + SSD_INNER + 2 * SSD_GROUPS * SSD_STATE
P1_DT = P1_GATES + 2 * D_MODEL
P1_COLS = P1_DT + SSD_GROUPS * LANE

VMEM_LIMIT = 56 * 1024 * 1024


def _cparams(sem):
    return pltpu.CompilerParams(dimension_semantics=sem, vmem_limit_bytes=VMEM_LIMIT)


def _split3(v):
    hi = v.astype(BF16)
    r1 = v - hi.astype(F32)
    mid = r1.astype(BF16)
    lo = (r1 - mid.astype(F32)).astype(BF16)
    return hi, mid, lo


def _dot(a, b):
    return jnp.dot(a, b, preferred_element_type=F32)


def _dot_nt(a, b):
    return lax.dot_general(a, b, (((1,), (1,)), ((), ())), preferred_element_type=F32)


def _silu(v):
    return v * jax.nn.sigmoid(v)


def _layer_norm(r, g, b):
    mu = jnp.mean(r, axis=-1, keepdims=True)
    d = r - mu
    var = jnp.mean(d * d, axis=-1, keepdims=True)
    return d * lax.rsqrt(var + LN_EPS) * g + b


def _adaln_kernel(c_ref, w_ref, b_ref, o_ref):
    s = _silu(c_ref[...]).astype(BF16)
    o_ref[0] = _dot(s, w_ref[0].astype(BF16)) + b_ref[0]


def adaln_all(c8, w, b):
    nl = w.shape[0]
    tn = 1024
    return pl.pallas_call(
        _adaln_kernel,
        grid=(nl, 3 * D_MODEL // tn),
        in_specs=[pl.BlockSpec((8, D_MODEL), lambda n, j: (0, 0)),
                  pl.BlockSpec((1, D_MODEL, tn), lambda n, j: (n, 0, j)),
                  pl.BlockSpec((1, 1, tn), lambda n, j: (n, 0, j))],
        out_specs=pl.BlockSpec((1, 8, tn), lambda n, j: (n, 0, j)),
        out_shape=jax.ShapeDtypeStruct((nl, 8, 3 * D_MODEL), F32),
        compiler_params=_cparams(("parallel", "parallel")),
        name="adaln",
    )(c8, w, b.reshape(nl, 1, 3 * D_MODEL))


def _rope_kernel(pos_ref, invf_ref, cos_ref, sin_ref):
    ang = pos_ref[...].astype(F32) * invf_ref[...]
    lane = lax.broadcasted_iota(I32, ang.shape, 1)
    sign = jnp.where((lane & 63) < 32, -1.0, 1.0)
    cos_ref[...] = jnp.cos(ang)
    sin_ref[...] = jnp.sin(ang) * sign


def rope_tables(pos_col, invf):
    t = pos_col.shape[0]
    tm = min(t, 2048)
    return pl.pallas_call(
        _rope_kernel,
        grid=(t // tm,),
        in_specs=[pl.BlockSpec((tm, 1), lambda i: (i, 0)),
                  pl.BlockSpec((1, LANE), lambda i: (0, 0))],
        out_specs=[pl.BlockSpec((tm, LANE), lambda i: (i, 0))] * 2,
        out_shape=[jax.ShapeDtypeStruct((t, LANE), F32)] * 2,
        compiler_params=_cparams(("parallel",)),
        name="rope_tables",
    )(pos_col, invf)


def _inproj_kernel(n_rope, x_ref, mod_ref, w_ref, *rest):
    if n_rope:
        cos_ref, sin_ref, o_ref, h_scr = rest
    else:
        o_ref, h_scr = rest
    j = pl.program_id(1)

    @pl.when(j == 0)
    def _():
        shift = mod_ref[0, :, 0:D_MODEL]
        scale = mod_ref[0, :, D_MODEL:2 * D_MODEL]
        h_scr[...] = (x_ref[...] * (1.0 + scale) + shift).astype(BF16)

    acc = _dot(h_scr[...], w_ref[...])
    if not n_rope:
        o_ref[...] = acc.astype(o_ref.dtype)
        return

    @pl.when(j < n_rope)
    def _():
        tn = acc.shape[1]
        reps = tn // LANE
        cos = jnp.concatenate([cos_ref[...]] * reps, axis=1)
        sin = jnp.concatenate([sin_ref[...]] * reps, axis=1)
        lane = lax.broadcasted_iota(I32, acc.shape, 1)
        swapped = jnp.where((lane & 63) < 32, pltpu.roll(acc, tn - 32, 1), pltpu.roll(acc, 32, 1))
        o_ref[...] = (acc * cos + swapped * sin).astype(o_ref.dtype)

    @pl.when(j >= n_rope)
    def _():
        o_ref[...] = acc.astype(o_ref.dtype)


def inproj(x, mod, w, seq, out_dtype, rope=None, n_rope_cols=0):
    t = x.shape[0]
    n = w.shape[1]
    tm = min(1024, seq)
    tn = 512
    n_rope = n_rope_cols // tn
    in_specs = [pl.BlockSpec((tm, D_MODEL), lambda i, j: (i, 0)),
                pl.BlockSpec((1, 1, 3 * D_MODEL), lambda i, j: ((i * tm) // seq, 0, 0)),
                pl.BlockSpec((D_MODEL, tn), lambda i, j: (0, j))]
    args = [x, mod, w]
    if n_rope:
        in_specs += [pl.BlockSpec((tm, LANE), lambda i, j: (i, 0))] * 2
        args += list(rope)
    return pl.pallas_call(
        functools.partial(_inproj_kernel, n_rope),
        grid=(t // tm, n // tn),
        in_specs=in_specs,
        out_specs=pl.BlockSpec((tm, tn), lambda i, j: (i, j)),
        out_shape=jax.ShapeDtypeStruct((t, n), out_dtype),
        scratch_shapes=[pltpu.VMEM((tm, D_MODEL), BF16)],
        compiler_params=_cparams(("parallel", "arbitrary")),
        name="inproj_rope" if n_rope else "inproj",
    )(*args)


def _ssd_kernel(z_ref, x_ref, b_ref, c_ref, dt_ref, cwx_ref, cwb_ref, cwc_ref, cbx_ref, cbb_ref, cbc_ref,
                dtb_ref, alog_ref, dsk_ref, nw_ref, tri_ref, exp_ref, o_ref,
                ex_scr, eb_scr, ec_scr, st_scr):
    L = SSD_CHUNK
    ci = pl.program_id(2)

    @pl.when(ci == 0)
    def _():
        ex_scr[0:8, :] = jnp.zeros((8, GROUP_W), F32)
        eb_scr[0:8, :] = jnp.zeros((8, SSD_STATE), F32)
        ec_scr[0:8, :] = jnp.zeros((8, SSD_STATE), F32)
        st_scr[...] = jnp.zeros_like(st_scr)

    def conv_silu(raw_ref, scr, w_ref, bias_ref):
        scr[8:8 + L, :] = raw_ref[...]
        acc = bias_ref[...] + w_ref[SSD_CONV - 1:SSD_CONV, :] * scr[8:8 + L, :]
        for k in range(SSD_CONV - 1):
            acc = acc + w_ref[k:k + 1, :] * scr[5 + k:5 + k + L, :]
        scr[0:8, :] = scr[L:L + 8, :]
        return _silu(acc)

    xs = conv_silu(x_ref, ex_scr, cwx_ref, cbx_ref)
    bs = conv_silu(b_ref, eb_scr, cwb_ref, cbb_ref)
    cc = conv_silu(c_ref, ec_scr, cwc_ref, cbc_ref)

    dtr = dt_ref[...] + dtb_ref[0]
    dtv = jnp.maximum(dtr, 0.0) + jnp.log1p(jnp.exp(-jnp.abs(dtr)))
    a_neg = -jnp.exp(alog_ref[0])
    dta = dtv * a_neg
    tri = tri_ref[...]
    h3 = _split3(dta)
    cs = _dot(tri, h3[0]) + _dot(tri, h3[1]) + _dot(tri, h3[2])
    cs_t = cs.T
    cs_last = cs[L - 1:L, :]
    ecs = jnp.exp(cs)
    dte = dtv * jnp.exp(cs_last - cs)
    dlast = jnp.broadcast_to(jnp.exp(cs_last), (8, LANE))

    expand = exp_ref[...]

    def widen(v):
        p = _split3(v)
        return _dot(p[0], expand) + _dot(p[1], expand) + _dot(p[2], expand)

    dt_x = widen(dtv)
    ecs_x = widen(ecs)
    dte_x = widen(dte)
    dlast_x = widen(dlast)[0:1, :]

    xdt = xs * dt_x
    xdt_b = xdt.astype(BF16)
    bs_b = bs.astype(BF16)
    cc_b = cc.astype(BF16)
    cb = _dot_nt(cc_b, bs_b)
    row = lax.broadcasted_iota(I32, (L, L), 0)
    col = lax.broadcasted_iota(I32, (L, L), 1)
    causal = row >= col
    lane = lax.broadcasted_iota(I32, (L, LANE), 1)
    first_head = lane < SSD_HEADDIM

    parts = []
    for p in range(SSD_HPG // 2):
        xp = xdt_b[:, LANE * p:LANE * (p + 1)]
        ys = []
        for hh in range(2):
            i = 2 * p + hh
            seg = cs[:, i:i + 1] - cs_t[i:i + 1, :]
            dec = jnp.exp(jnp.where(causal, seg, -jnp.inf))
            ys.append(_dot((cb * dec).astype(BF16), xp))
        parts.append(jnp.where(first_head, ys[0], ys[1]))
    y = jnp.concatenate(parts, axis=1)

    st = st_scr[...]
    y = y + _dot(cc_b, st.astype(BF16)) * ecs_x
    xw = (xs * dte_x).astype(BF16)
    st_scr[...] = st * dlast_x + _dot(bs.T.astype(BF16), xw)

    y = y + dsk_ref[...] * xs
    gy = y * _silu(z_ref[...])
    ms = jnp.mean(gy * gy, axis=-1, keepdims=True)
    o_ref[...] = (gy * lax.rsqrt(ms + LN_EPS) * nw_ref[...]).astype(o_ref.dtype)


def ssd_mixer(p1, conv_w, conv_b, dtb, alog, dsk, nw, tri, expand, batch, seq):
    t = p1.shape[0]
    L = SSD_CHUNK
    nc = seq // L
    xb0 = P1_XBC // GROUP_W
    bb0 = (P1_XBC + SSD_INNER) // SSD_STATE
    cb0 = bb0 + SSD_GROUPS
    db0 = P1_DT // LANE
    wb0 = SSD_INNER // SSD_STATE
    rowmap = lambda b, g, c: b * nc + c
    in_specs = [
        pl.BlockSpec((L, GROUP_W), lambda b, g, c: (rowmap(b, g, c), g)),
        pl.BlockSpec((L, GROUP_W), lambda b, g, c: (rowmap(b, g, c), xb0 + g)),
        pl.BlockSpec((L, SSD_STATE), lambda b, g, c: (rowmap(b, g, c), bb0 + g)),
        pl.BlockSpec((L, SSD_STATE), lambda b, g, c: (rowmap(b, g, c), cb0 + g)),
        pl.BlockSpec((L, LANE), lambda b, g, c: (rowmap(b, g, c), db0 + g)),
        pl.BlockSpec((SSD_CONV, GROUP_W), lambda b, g, c: (0, g)),
        pl.BlockSpec((SSD_CONV, SSD_STATE), lambda b, g, c: (0, wb0 + g)),
        pl.BlockSpec((SSD_CONV, SSD_STATE), lambda b, g, c: (0, wb0 + SSD_GROUPS + g)),
        pl.BlockSpec((1, GROUP_W), lambda b, g, c: (0, g)),
        pl.BlockSpec((1, SSD_STATE), lambda b, g, c: (0, wb0 + g)),
        pl.BlockSpec((1, SSD_STATE), lambda b, g, c: (0, wb0 + SSD_GROUPS + g)),
        pl.BlockSpec((1, 1, LANE), lambda b, g, c: (g, 0, 0)),
        pl.BlockSpec((1, 1, LANE), lambda b, g, c: (g, 0, 0)),
        pl.BlockSpec((1, GROUP_W), lambda b, g, c: (0, g)),
        pl.BlockSpec((1, GROUP_W), lambda b, g, c: (0, g)),
        pl.BlockSpec((L, L), lambda b, g, c: (0, 0)),
        pl.BlockSpec((LANE, GROUP_W), lambda b, g, c: (0, 0)),
    ]
    return pl.pallas_call(
        _ssd_kernel,
        grid=(batch, SSD_GROUPS, nc),
        in_specs=in_specs,
        out_specs=pl.BlockSpec((L, GROUP_W), lambda b, g, c: (rowmap(b, g, c), g)),
        out_shape=jax.ShapeDtypeStruct((t, SSD_INNER), BF16),
        scratch_shapes=[pltpu.VMEM((L + 8, GROUP_W), F32), pltpu.VMEM((L + 8, SSD_STATE), F32),
                        pltpu.VMEM((L + 8, SSD_STATE), F32), pltpu.VMEM((SSD_STATE, GROUP_W), F32)],
        compiler_params=_cparams(("parallel", "parallel", "arbitrary")),
        name="ssd",
    )(p1, p1, p1, p1, p1, conv_w, conv_w, conv_w, conv_b, conv_b, conv_b, dtb, alog, dsk, nw, tri, expand)


MOBA_ROWS = 128
MOBA_UNROLL = 4


def _moba_kernel(nb, q_ref, k_ref, v_ref, o_ref, km_scr, kt_scr, vx_scr, lhs_scr, m_scr, acc_scr, s_scr):
    BS = MOBA_BLOCK
    R = MOBA_ROWS
    qi = pl.program_id(2)

    @pl.when(qi == 0)
    def _():
        km_scr[...] = jnp.zeros_like(km_scr)
        sub = lax.broadcasted_iota(I32, (LANE, BS), 0)

        def body(j, carry):
            j0 = pl.multiple_of(j * BS, BS)
            kb = k_ref[pl.ds(j0, BS), :].astype(F32)
            km_scr[pl.ds(j, 1), :] = jnp.mean(kb, axis=0, keepdims=True)
            kt_scr[j, 0:LANE, :] = kb.T.astype(BF16)
            kt_scr[j, LANE:2 * LANE, :] = jnp.where(sub == j, 1.0, 0.0).astype(BF16)
            vx_scr[pl.ds(j0, BS), 0:LANE] = v_ref[pl.ds(j0, BS), :]
            vx_scr[pl.ds(j0, BS), LANE:2 * LANE] = jnp.ones((BS, LANE), BF16)
            return carry

        lax.fori_loop(0, nb, body, 0)

    q = q_ref[...].astype(F32)
    lane = lax.broadcasted_iota(I32, (BS, LANE), 1)
    is_a = lane < ATTN_HEAD_DIM
    q_heads = (jnp.where(is_a, q, 0.0).astype(BF16), jnp.where(is_a, 0.0, q).astype(BF16))
    ns = km_scr.shape[0]
    kmb = km_scr[...].astype(BF16)
    slot = lax.broadcasted_iota(I32, (ns, BS), 0)
    slot_f = slot.astype(F32)

    def neg_mask(qh):
        g = _dot_nt(kmb, qh)
        g = jnp.where(slot < qi, g, -jnp.inf)
        sel = jnp.zeros((ns, BS), F32)
        for _ in range(MOBA_TOPK):
            m = jnp.max(g, axis=0, keepdims=True)
            hit = jnp.logical_and(g == m, m > -jnp.inf)
            idx = jnp.min(jnp.where(hit, slot_f, float(LANE)), axis=0, keepdims=True)
            pick = slot_f == idx
            sel = jnp.where(pick, 1.0, sel)
            g = jnp.where(pick, -jnp.inf, g)
        nm_t = (sel - 1.0) * (-NEG_BIG)
        if ns < LANE:
            nm_t = jnp.concatenate([nm_t, jnp.zeros((LANE - ns, BS), F32)], axis=0)
        return nm_t.T.astype(BF16)

    for h in range(2):
        lhs_scr[h * BS:(h + 1) * BS, 0:LANE] = q_heads[h]
        lhs_scr[h * BS:(h + 1) * BS, LANE:2 * LANE] = neg_mask(q_heads[h])

    own0 = pl.multiple_of(qi * BS, BS)
    k_own = kt_scr[qi, 0:LANE, :]
    v_own = vx_scr[pl.ds(own0, BS), :]
    coli = lax.broadcasted_iota(I32, (R, BS), 1)
    rowi = lax.broadcasted_iota(I32, (R, BS), 0)
    for r0 in range(0, 2 * BS, R):
        s = _dot(lhs_scr[r0:r0 + R, 0:LANE], k_own)
        s = jnp.where(rowi + (r0 % BS) >= coli, s, NEG_BIG)
        m = jnp.max(s, axis=1, keepdims=True)
        p = jnp.exp2(s - m)
        acc_scr[r0:r0 + R, :] = _dot(p.astype(BF16), v_own)
        m_scr[r0:r0 + R, :] = jnp.broadcast_to(m, (R, LANE))

    s_scr[...] = _dot(lhs_scr[...], kt_scr[0])

    def one_block(j):
        kt_next = kt_scr[jnp.minimum(j + 1, nb - 1)]
        vx = vx_scr[pl.ds(pl.multiple_of(j * BS, BS), BS), :]
        for r0 in range(0, 2 * BS, R):
            s = s_scr[r0:r0 + R, :]
            s_scr[r0:r0 + R, :] = _dot(lhs_scr[r0:r0 + R, :], kt_next)
            m_old = m_scr[r0:r0 + R, :]
            m_new = jnp.maximum(m_old, jnp.max(s, axis=1, keepdims=True))
            a = jnp.exp2(m_old - m_new)
            p = jnp.exp2(s - jnp.concatenate([m_new, m_new], axis=1))
            pv = _dot(p.astype(BF16), vx)
            acc_scr[r0:r0 + R, :] = jnp.concatenate([a, a], axis=1) * acc_scr[r0:r0 + R, :] + pv
            m_scr[r0:r0 + R, :] = m_new

    def body(t, carry):
        for u in range(MOBA_UNROLL):
            one_block(jnp.minimum(MOBA_UNROLL * t + u, nb - 1))
        return carry

    lax.fori_loop(0, (qi + MOBA_UNROLL - 1) // MOBA_UNROLL, body, 0)
    out_a = acc_scr[0:BS, 0:LANE] / acc_scr[0:BS, LANE:2 * LANE]
    out_b = acc_scr[BS:2 * BS, 0:LANE] / acc_scr[BS:2 * BS, LANE:2 * LANE]
    o_ref[...] = jnp.where(is_a, out_a, out_b).astype(o_ref.dtype)


def moba_attention(qkv, batch, seq):
    t = qkv.shape[0]
    BS = MOBA_BLOCK
    nb = seq // BS
    hp = ATTN_HEADS // 2
    return pl.pallas_call(
        functools.partial(_moba_kernel, nb),
        grid=(batch, hp, nb),
        in_specs=[pl.BlockSpec((BS, LANE), lambda b, h, i: (b * nb + i, h)),
                  pl.BlockSpec((seq, LANE), lambda b, h, i: (b, hp + h)),
                  pl.BlockSpec((seq, LANE), lambda b, h, i: (b, 2 * hp + h))],
        out_specs=pl.BlockSpec((BS, LANE), lambda b, h, i: (b * nb + i, h)),
        out_shape=jax.ShapeDtypeStruct((t, ATTN_WIDTH), BF16),
        scratch_shapes=[pltpu.VMEM((-(-nb // 8) * 8, LANE), F32),
                        pltpu.VMEM((nb, 2 * LANE, BS), BF16),
                        pltpu.VMEM((seq, 2 * LANE), BF16),
                        pltpu.VMEM((2 * BS, 2 * LANE), BF16),
                        pltpu.VMEM((2 * BS, LANE), F32),
                        pltpu.VMEM((2 * BS, 2 * LANE), F32),
                        pltpu.VMEM((2 * BS, BS), F32)],
        compiler_params=_cparams(("parallel", "parallel", "arbitrary")),
        name="moba",
    )(qkv, qkv, qkv)


def _mixout_kernel(ys_ref, ya_ref, gs_ref, ga_ref, x_ref, mod_ref, wbs_ref, wba_ref, wo_ref, g_ref, b_ref, o_ref):
    a = _dot(ys_ref[...], wbs_ref[...])
    bb = _dot(ya_ref[...], wba_ref[...])
    merged = jax.nn.sigmoid(gs_ref[...]) * a + jax.nn.sigmoid(ga_ref[...]) * bb
    y = _dot(merged.astype(BF16), wo_ref[...])
    gate = mod_ref[0, :, 2 * D_MODEL:3 * D_MODEL]
    r = DEEPNORM_ALPHA * x_ref[...] + (1.0 + gate) * y
    o_ref[...] = _layer_norm(r, g_ref[...], b_ref[...])


def mix_out(y_ssd, y_attn, p1, x, mod, wbs, wba, wo, ln_g, ln_b, seq):
    t = x.shape[0]
    tm = min(512, seq)
    gb = P1_GATES // D_MODEL
    full = lambda shape: pl.BlockSpec(shape, lambda i: (0, 0))
    return pl.pallas_call(
        _mixout_kernel,
        grid=(t // tm,),
        in_specs=[pl.BlockSpec((tm, SSD_INNER), lambda i: (i, 0)),
                  pl.BlockSpec((tm, ATTN_WIDTH), lambda i: (i, 0)),
                  pl.BlockSpec((tm, D_MODEL), lambda i: (i, gb)),
                  pl.BlockSpec((tm, D_MODEL), lambda i: (i, gb + 1)),
                  pl.BlockSpec((tm, D_MODEL), lambda i: (i, 0)),
                  pl.BlockSpec((1, 1, 3 * D_MODEL), lambda i: ((i * tm) // seq, 0, 0)),
                  full((SSD_INNER, D_MODEL)), full((ATTN_WIDTH, D_MODEL)), full((D_MODEL, D_MODEL)),
                  full((1, D_MODEL)), full((1, D_MODEL))],
        out_specs=pl.BlockSpec((tm, D_MODEL), lambda i: (i, 0)),
        out_shape=jax.ShapeDtypeStruct((t, D_MODEL), F32),
        compiler_params=_cparams(("parallel",)),
        name="mix_out",
    )(y_ssd, y_attn, p1, p1, x, mod, wbs, wba, wo, ln_g, ln_b)


def _router_kernel(x_ref, mod_ref, wr_ref, br_ref, tri_ref, h_ref, ri_ref, rw_ref, cnt_ref, base_scr):
    i = pl.program_id(0)

    @pl.when(i == 0)
    def _():
        base_scr[...] = jnp.zeros_like(base_scr)

    shift = mod_ref[0, :, 0:D_MODEL]
    scale = mod_ref[0, :, D_MODEL:2 * D_MODEL]
    h = x_ref[...] * (1.0 + scale) + shift
    h_ref[...] = h
    logits = _dot(h.astype(BF16), wr_ref[...]) + br_ref[...]
    tm = logits.shape[0]
    lane_i = lax.broadcasted_iota(I32, (tm, LANE), 1)
    lane = lane_i.astype(F32)
    neg = -jnp.inf

    def first_max(v):
        m = jnp.max(v, axis=1, keepdims=True)
        idx = jnp.min(jnp.where(v == m, lane, float(LANE)), axis=1, keepdims=True)
        return m, idx

    lg = jnp.where(lane < MOE_GROUPS, logits, neg)
    ex = jnp.exp(lg - jnp.max(lg, axis=1, keepdims=True))
    prob = ex / jnp.sum(ex, axis=1, keepdims=True)
    g_val, g_idx = first_max(jnp.where(lane < MOE_GROUPS, prob, neg))
    lo = EXPERT_LANE0 + MOE_EPG * g_idx
    el = jnp.where(jnp.logical_and(lane >= lo, lane < lo + MOE_EPG), logits, neg)
    v1, i1 = first_max(el)
    v2, i2 = first_max(jnp.where(lane == i1, neg, el))
    e2 = jnp.exp(v2 - v1)
    den = 1.0 + e2
    w1 = (1.0 / den) * g_val
    w2 = (e2 / den) * g_val
    e_1 = i1 - EXPERT_LANE0
    e_2 = i2 - EXPERT_LANE0

    onehot = jnp.where(lane == e_1, 1.0, 0.0) + jnp.where(lane == e_2, 1.0, 0.0)
    before = _dot(tri_ref[...], onehot.astype(BF16)) + base_scr[0:1, :]
    r1 = jnp.sum(jnp.where(lane == e_1, before, 0.0), axis=1, keepdims=True)
    r2 = jnp.sum(jnp.where(lane == e_2, before, 0.0), axis=1, keepdims=True)
    base_scr[...] = base_scr[...] + jnp.sum(onehot, axis=0, keepdims=True)
    cnt_ref[...] = base_scr[...].astype(I32)

    ri = jnp.where(lane == 0, e_1, jnp.where(lane == 1, e_2, jnp.where(lane == 2, r1, jnp.where(lane == 3, r2, 0.0))))
    ri_ref[...] = ri.astype(I32)
    rw_ref[...] = jnp.where(lane == 0, w1, jnp.where(lane == 1, w2, 0.0))


def moe_router(x, mod, wr, br, tri, seq):
    t = x.shape[0]
    tm = tri.shape[0]
    return pl.pallas_call(
        _router_kernel,
        grid=(t // tm,),
        in_specs=[pl.BlockSpec((tm, D_MODEL), lambda i: (i, 0)),
                  pl.BlockSpec((1, 1, 3 * D_MODEL), lambda i: ((i * tm) // seq, 0, 0)),
                  pl.BlockSpec((D_MODEL, LANE), lambda i: (0, 0)),
                  pl.BlockSpec((1, LANE), lambda i: (0, 0)),
                  pl.BlockSpec((tm, tm), lambda i: (0, 0))],
        out_specs=[pl.BlockSpec((tm, D_MODEL), lambda i: (i, 0)),
                   pl.BlockSpec((tm, LANE), lambda i: (i, 0)),
                   pl.BlockSpec((tm, LANE), lambda i: (i, 0)),
                   pl.BlockSpec((8, LANE), lambda i: (0, 0))],
        out_shape=[jax.ShapeDtypeStruct((t, D_MODEL), F32),
                   jax.ShapeDtypeStruct((t, LANE), I32),
                   jax.ShapeDtypeStruct((t, LANE), F32),
                   jax.ShapeDtypeStruct((8, LANE), I32)],
        scratch_shapes=[pltpu.VMEM((8, LANE), F32)],
        compiler_params=_cparams(("arbitrary",)),
        name="moe_router",
    )(x, mod, wr, br, tri)


DISPATCH_TOKENS = 512


def _dispatch_kernel(pos_ref, h_ref, xr_in, xr_hbm, sem):
    del xr_in
    tm = DISPATCH_TOKENS

    def row_copy(t, dst):
        return pltpu.make_async_copy(h_ref.at[pl.ds(t, 1)], xr_hbm.at[pl.ds(dst, 1)], sem)

    def start(t, carry):
        row_copy(t, pos_ref[0, 0, 2 * t]).start()
        row_copy(t, pos_ref[0, 0, 2 * t + 1]).start()
        return carry

    lax.fori_loop(0, tm, start, 0, unroll=8)

    def wait(t, carry):
        row_copy(0, 0).wait()
        row_copy(0, 0).wait()
        return carry

    lax.fori_loop(0, tm, wait, 0, unroll=8)


def moe_dispatch(h, pos, n_rows):
    t = h.shape[0]
    tm = DISPATCH_TOKENS
    zeros = jnp.zeros((n_rows, D_MODEL), F32)
    return pl.pallas_call(
        _dispatch_kernel,
        grid=(t // tm,),
        in_specs=[pl.BlockSpec((1, 1, 2 * tm), lambda i: (i, 0, 0), memory_space=pltpu.SMEM),
                  pl.BlockSpec((tm, D_MODEL), lambda i: (i, 0)),
                  pl.BlockSpec(memory_space=pl.ANY)],
        out_specs=pl.BlockSpec(memory_space=pl.ANY),
        out_shape=jax.ShapeDtypeStruct((n_rows, D_MODEL), F32),
        scratch_shapes=[pltpu.SemaphoreType.DMA(())],
        input_output_aliases={2: 0},
        compiler_params=_cparams(("arbitrary",)),
        name="moe_dispatch",
    )(pos.reshape(t // tm, 1, 2 * tm), h, zeros)


def _expert_kernel(be_ref, nu_ref, x_ref, wg_ref, wu_ref, wd_ref, o_ref):
    i = pl.program_id(0)

    @pl.when(i < nu_ref[0])
    def _():
        xb = x_ref[...].astype(BF16)
        hmid = _silu(_dot(xb, wg_ref[0])) * _dot(xb, wu_ref[0])
        o_ref[...] = _dot(hmid.astype(BF16), wd_ref[0])

    @pl.when(i >= nu_ref[0])
    def _():
        o_ref[...] = jnp.zeros_like(o_ref)


def moe_experts(x_rows, block_e, n_used, wg, wu, wd):
    n_rows = x_rows.shape[0]
    nblk = n_rows // ROW_BLOCK
    rowmap = lambda i, be, nu: (jnp.minimum(i, nu[0] - 1), 0)
    grid_spec = pltpu.PrefetchScalarGridSpec(
        num_scalar_prefetch=2,
        grid=(nblk,),
        in_specs=[pl.BlockSpec((ROW_BLOCK, D_MODEL), rowmap),
                  pl.BlockSpec((1, D_MODEL, EXPERT_FF), lambda i, be, nu: (be[i], 0, 0)),
                  pl.BlockSpec((1, D_MODEL, EXPERT_FF), lambda i, be, nu: (be[i], 0, 0)),
                  pl.BlockSpec((1, EXPERT_FF, D_MODEL), lambda i, be, nu: (be[i], 0, 0))],
        out_specs=pl.BlockSpec((ROW_BLOCK, D_MODEL), lambda i, be, nu: (i, 0)),
    )
    return pl.pallas_call(
        _expert_kernel,
        grid_spec=grid_spec,
        out_shape=jax.ShapeDtypeStruct((n_rows, D_MODEL), F32),
        compiler_params=_cparams(("arbitrary",)),
        name="moe_experts",
    )(block_e, n_used, x_rows, wg, wu, wd)


COMBINE_TOKENS = 256


def _combine_kernel(pos_ref, yr_hbm, rw_ref, x_ref, mod_ref, g_ref, b_ref, o_ref, buf0, buf1, sem):
    tm = COMBINE_TOKENS

    def row_copy(src, buf, t):
        return pltpu.make_async_copy(yr_hbm.at[pl.ds(src, 1)], buf.at[pl.ds(t, 1)], sem)

    def start(t, carry):
        row_copy(pos_ref[0, 0, 2 * t], buf0, t).start()
        row_copy(pos_ref[0, 0, 2 * t + 1], buf1, t).start()
        return carry

    lax.fori_loop(0, tm, start, 0, unroll=8)

    def wait(t, carry):
        row_copy(0, buf0, 0).wait()
        row_copy(0, buf1, 0).wait()
        return carry

    lax.fori_loop(0, tm, wait, 0, unroll=8)

    w = rw_ref[...]
    y = w[:, 0:1] * buf0[...] + w[:, 1:2] * buf1[...]
    gate = mod_ref[0, :, 2 * D_MODEL:3 * D_MODEL]
    r = DEEPNORM_ALPHA * x_ref[...] + (1.0 + gate) * y
    o_ref[...] = _layer_norm(r, g_ref[...], b_ref[...])


def moe_combine(y_rows, pos, rw, x, mod, ln_g, ln_b, seq):
    t = x.shape[0]
    tm = COMBINE_TOKENS
    return pl.pallas_call(
        _combine_kernel,
        grid=(t // tm,),
        in_specs=[pl.BlockSpec((1, 1, 2 * tm), lambda i: (i, 0, 0), memory_space=pltpu.SMEM),
                  pl.BlockSpec(memory_space=pl.ANY),
                  pl.BlockSpec((tm, LANE), lambda i: (i, 0)),
                  pl.BlockSpec((tm, D_MODEL), lambda i: (i, 0)),
                  pl.BlockSpec((1, 1, 3 * D_MODEL), lambda i: ((i * tm) // seq, 0, 0)),
                  pl.BlockSpec((1, D_MODEL), lambda i: (0, 0)),
                  pl.BlockSpec((1, D_MODEL), lambda i: (0, 0))],
        out_specs=pl.BlockSpec((tm, D_MODEL), lambda i: (i, 0)),
        out_shape=jax.ShapeDtypeStruct((t, D_MODEL), F32),
        scratch_shapes=[pltpu.VMEM((tm, D_MODEL), F32), pltpu.VMEM((tm, D_MODEL), F32),
                        pltpu.SemaphoreType.DMA(())],
        compiler_params=_cparams(("arbitrary",)),
        name="moe_combine",
    )(pos.reshape(t // tm, 1, 2 * tm), y_rows, rw, x, mod, ln_g, ln_b)


def moe_sublayer(x, mod, wr, br, tri, wg, wu, wd, ln_g, ln_b, seq):
    t = x.shape[0]
    h, ri, rw, cnt = moe_router(x, mod, wr, br, tri, seq)
    counts = cnt[0, :N_EXPERTS]
    padded = (counts + ROW_BLOCK - 1) // ROW_BLOCK * ROW_BLOCK
    ends = jnp.cumsum(padded)
    pstarts = ends - padded
    n_rows = t * 2 + N_EXPERTS * ROW_BLOCK
    nblk = n_rows // ROW_BLOCK
    blk0 = jnp.arange(nblk, dtype=I32) * ROW_BLOCK
    block_e = jnp.minimum(jnp.sum((ends[None, :] <= blk0[:, None]).astype(I32), axis=1), N_EXPERTS - 1)
    n_used = (ends[-1] // ROW_BLOCK).astype(I32).reshape(1)
    eid = ri[:, 0:2]
    rank = ri[:, 2:4]
    start_of = jnp.sum(jnp.where(eid[:, :, None] == jnp.arange(N_EXPERTS, dtype=I32), pstarts, 0), axis=-1)
    pos = (start_of + rank).reshape(t * 2).astype(I32)
    x_rows = moe_dispatch(h, pos, n_rows)
    y_rows = moe_experts(x_rows, block_e.astype(I32), n_used, wg, wu, wd)
    return moe_combine(y_rows, pos, rw, x, mod, ln_g, ln_b, seq)


def kernel(x, c, positions, w_in, conv_w, conv_b, dt_bias, a_log, d_skip, ssd_norm_w, w_branch_ssd, w_branch_attn, w_out, w_ada_mix, b_ada_mix, ln_mix_g, ln_mix_b, w_ada_ffn, b_ada_ffn, w_router_group, b_router_group, w_router_expert, b_router_expert, w_expert_gate, w_expert_up, w_expert_down, ln_ffn_g, ln_ffn_b):
    batch, seq, d = x.shape
    t = batch * seq
    nl = w_in.shape[0]
    xf = x.reshape(t, d)

    c8 = jnp.zeros((8, d), F32).at[:batch].set(c)
    mod_mix = adaln_all(c8, w_ada_mix, b_ada_mix)[:, :batch, None, :]
    mod_ffn = adaln_all(c8, w_ada_ffn, b_ada_ffn)[:, :batch, None, :]
    inv_freq = ROPE_THETA ** (-jnp.arange(0, ATTN_HEAD_DIM, 2, dtype=F32) / ATTN_HEAD_DIM)
    invf = jnp.tile(inv_freq, LANE // 32)[None, :]
    rope = rope_tables(positions.reshape(t, 1).astype(I32), invf)

    tri_chunk = jnp.tril(jnp.ones((SSD_CHUNK, SSD_CHUNK), F32)).astype(BF16)
    expand = (jnp.arange(LANE)[:, None] == jnp.arange(GROUP_W)[None, :] // SSD_HEADDIM).astype(BF16)
    router_tm = min(512, seq)
    tri_router = jnp.tril(jnp.ones((router_tm, router_tm), F32), k=-1).astype(BF16)

    s_z, s_xbc, s_dt, s_qkv = SSD_INNER, SSD_INNER + 3072, SSD_INNER + 3072 + SSD_HEADS, SSD_INNER + 3072 + SSD_HEADS + 3 * ATTN_WIDTH
    head_lane = jnp.arange(SSD_HEADS) % SSD_HPG + (jnp.arange(SSD_HEADS) // SSD_HPG) * LANE

    for l in range(nl):
        wl = w_in[l]
        w_dt = jnp.zeros((d, SSD_GROUPS * LANE), F32).at[:, head_lane].set(wl[:, s_xbc:s_dt])
        w_p1 = jnp.concatenate([wl[:, :s_xbc], wl[:, s_qkv:], w_dt], axis=1).astype(BF16)
        qscale = jnp.concatenate([jnp.full((ATTN_WIDTH,), ATTN_HEAD_DIM ** -0.5 * math.log2(math.e), F32),
                                  jnp.ones((2 * ATTN_WIDTH,), F32)])
        w_qkv = (wl[:, s_dt:s_qkv] * qscale).astype(BF16)
        dtb = jnp.zeros((SSD_GROUPS, 1, LANE), F32).at[:, 0, :SSD_HPG].set(dt_bias[l].reshape(SSD_GROUPS, SSD_HPG))
        alog = jnp.zeros((SSD_GROUPS, 1, LANE), F32).at[:, 0, :SSD_HPG].set(a_log[l].reshape(SSD_GROUPS, SSD_HPG))
        dsk = jnp.repeat(d_skip[l], SSD_HEADDIM)[None, :]

        p1 = inproj(xf, mod_mix[l], w_p1, seq, F32)
        qkv = inproj(xf, mod_mix[l], w_qkv, seq, BF16, rope=rope, n_rope_cols=2 * ATTN_WIDTH)
        y_ssd = ssd_mixer(p1, conv_w[l], conv_b[l][None, :], dtb, alog, dsk, ssd_norm_w[l][None, :],
                          tri_chunk, expand, batch, seq)
        y_attn = moba_attention(qkv, batch, seq)
        xf = mix_out(y_ssd, y_attn, p1, xf, mod_mix[l], w_branch_ssd[l].astype(BF16), w_branch_attn[l].astype(BF16),
                     w_out[l].astype(BF16), ln_mix_g[l][None, :], ln_mix_b[l][None, :], seq)

        wr = jnp.zeros((d, LANE), F32).at[:, :MOE_GROUPS].set(w_router_group[l])
        wr = wr.at[:, EXPERT_LANE0:EXPERT_LANE0 + N_EXPERTS].set(w_router_expert[l]).astype(BF16)
        br = jnp.zeros((1, LANE), F32).at[0, :MOE_GROUPS].set(b_router_group[l])
        br = br.at[0, EXPERT_LANE0:EXPERT_LANE0 + N_EXPERTS].set(b_router_expert[l])
        xf = moe_sublayer(xf, mod_ffn[l], wr, br, tri_router, w_expert_gate[l].astype(BF16), w_expert_up[l].astype(BF16),
                          w_expert_down[l].astype(BF16), ln_ffn_g[l][None, :], ln_ffn_b[l][None, :], seq)

    return xf.reshape(batch, seq, d)
```
